```python
import math
import jax
import jax.numpy as jnp
from jax import lax
import numpy as np

D_MODEL = 1024
BATCH = 8
SEQ = 4096
DEPTH = 2

SB_HEADS = 8
SB_HEAD_DIM = 64
SB_BLOCK = 128
SB_WIDTH = SB_HEADS * SB_HEAD_DIM
DN_HEADS = 4
DN_HEAD_DIM = 128
DN_WIDTH = DN_HEADS * DN_HEAD_DIM
DN_CONV = 4
DN_CHUNK = 64
IN_SPLITS = [SB_WIDTH, SB_WIDTH, SB_WIDTH, 3 * DN_WIDTH, DN_HEADS, DN_HEADS, DN_WIDTH]
IN_DIM_EVEN = sum(IN_SPLITS)
MIX_WIDTH_EVEN = SB_WIDTH + DN_WIDTH
S5_GROUP = 16
S5_GROUPS = D_MODEL // S5_GROUP
S5_STATE = 64
S5_CHUNK = 128
S5_MIN_NEG = -1e-4
MOE_GROUPS = 4
MOE_PER_GROUP = 8
N_EXPERTS = MOE_GROUPS * MOE_PER_GROUP
EXPERT_FF = 256
MOE_TOPK = 2
N_EVEN = (DEPTH + 1) // 2
N_ODD = DEPTH // 2
RMS_EPS = 1e-6
L2_EPS = 1e-6

kernel_name = 'hybrid_stickbreak_deltanet_s5_hmoe'


def rmsnorm(x, g):
    xf = x.astype(jnp.float32)
    y = xf * lax.rsqrt(jnp.mean(xf * xf, axis=-1, keepdims=True) + RMS_EPS)
    return (y * g.astype(jnp.float32)).astype(x.dtype)


def l2norm(x):
    return x * lax.rsqrt(jnp.sum(x * x, axis=-1, keepdims=True) + L2_EPS)


def stick_breaking_attention(q, k, v):
    b, l, h, d = q.shape
    nb = l // SB_BLOCK
    qf = q.astype(jnp.float32) * (1.0 / math.sqrt(d))
    kf = k.astype(jnp.float32)
    vf = v.astype(jnp.float32)
    q_blocks = qf.reshape(b, nb, SB_BLOCK, h, d).transpose(1, 0, 3, 2, 4)
    key_pos = jnp.arange(l)

    def block(args):
        qb, i = args
        z = jnp.einsum('bhqd,bshd->bhqs', qb, kf)
        q_pos = i * SB_BLOCK + jnp.arange(SB_BLOCK)
        causal = key_pos[None, :] < q_pos[:, None]
        log_1m = jnp.where(causal, jax.nn.log_sigmoid(-z), 0.0)
        after = lax.cumsum(log_1m, axis=3, reverse=True) - log_1m
        log_w = jnp.where(causal, jax.nn.log_sigmoid(z) + after, -jnp.inf)
        return jnp.einsum('bhqs,bshd->bqhd', jnp.exp(log_w), vf)

    out = lax.map(block, (q_blocks, jnp.arange(nb)))
    return out.transpose(1, 0, 2, 3, 4).reshape(b, l, h * d)


def causal_depthwise_conv(x, w):
    kk, c = w.shape
    return lax.conv_general_dilated(
        x, w.reshape(kk, 1, c).astype(x.dtype), window_strides=(1,),
        padding=[(kk - 1, 0)], dimension_numbers=('NWC', 'WIO', 'NWC'),
        feature_group_count=c)


def gated_delta_rule(q, k, v, beta, g):
    b, h, l, dk = q.shape
    dv = v.shape[-1]
    c = DN_CHUNK
    n = l // c
    rs = lambda t: t.reshape((b, h, n, c) + t.shape[3:])
    q, k, v, beta, g = rs(q), rs(k), rs(v), rs(beta), rs(g)
    decay = jnp.cumsum(g, axis=-1)
    idx = jnp.arange(c)
    incl = idx[:, None] >= idx[None, :]
    strict = idx[:, None] > idx[None, :]
    dmask = jnp.exp(jnp.where(incl, decay[..., :, None] - decay[..., None, :], -jnp.inf))
    k_beta = k * beta[..., None]
    m = jnp.where(strict, jnp.einsum('bhnid,bhnjd->bhnij', k_beta, k) * dmask, 0.0)
    rhs = jnp.concatenate([k_beta * jnp.exp(decay)[..., None], v * beta[..., None]], axis=-1)
    sol = lax.linalg.triangular_solve(m + jnp.eye(c, dtype=m.dtype), rhs,
                                      left_side=True, lower=True)
    w, u = sol[..., :dk], sol[..., dk:]
    attn = jnp.einsum('bhnid,bhnjd->bhnij', q, k) * dmask
    q_dec = q * jnp.exp(decay)[..., None]
    k_tail = k * jnp.exp(decay[..., -1:] - decay)[..., None]
    chunk_dec = jnp.exp(decay[..., -1])

    def step(state, xs):
        w_n, u_n, attn_n, qd_n, kt_n, cd_n = xs
        v_new = u_n - jnp.einsum('bhcd,bhde->bhce', w_n, state)
        o = (jnp.einsum('bhcd,bhde->bhce', qd_n, state)
             + jnp.einsum('bhij,bhje->bhie', attn_n, v_new))
        state = state * cd_n[..., None, None] + jnp.einsum('bhcd,bhce->bhde', kt_n, v_new)
        return state, o

    xs = tuple(jnp.moveaxis(t, 2, 0) for t in (w, u, attn, q_dec, k_tail, chunk_dec))
    s0 = jnp.zeros((b, h, dk, dv), jnp.float32)
    _, o = lax.scan(step, s0, xs)
    return jnp.moveaxis(o, 0, 2).reshape(b, h, l, dv)


def gated_deltanet(qkv, beta_logit, a_logit, gate, conv_w, a_log, dt_bias, out_norm):
    b, l, _ = qkv.shape
    qkv = jax.nn.silu(causal_depthwise_conv(qkv, conv_w)).astype(jnp.float32)
    q, k, v = jnp.split(qkv, 3, axis=-1)
    heads = lambda t: t.reshape(b, l, DN_HEADS, DN_HEAD_DIM).transpose(0, 2, 1, 3)
    q = l2norm(heads(q)) * (DN_HEAD_DIM ** -0.5)
    k = l2norm(heads(k))
    v = heads(v)
    beta = jax.nn.sigmoid(beta_logit.astype(jnp.float32)).transpose(0, 2, 1)
    g = (-jnp.exp(a_log.astype(jnp.float32))
         * jax.nn.softplus(a_logit.astype(jnp.float32) + dt_bias.astype(jnp.float32)))
    g = g.transpose(0, 2, 1)
    o = gated_delta_rule(q, k, v, beta, g).transpose(0, 2, 1, 3)
    z = gate.astype(jnp.float32).reshape(b, l, DN_HEADS, DN_HEAD_DIM)
    o = rmsnorm(o, out_norm) * jax.nn.silu(z)
    return o.reshape(b, l, DN_WIDTH)


def even_mixer(h, w_in, conv_w, a_log, dt_bias, out_norm, w_out):
    b, l, _ = h.shape
    proj = h @ w_in
    cuts = np.cumsum(IN_SPLITS)[:-1].tolist()
    sb_q, sb_k, sb_v, dn_qkv, dn_beta, dn_a, dn_gate = jnp.split(proj, cuts, axis=-1)
    heads = lambda t: t.reshape(b, l, SB_HEADS, SB_HEAD_DIM)
    y_sb = stick_breaking_attention(heads(sb_q), heads(sb_k), heads(sb_v))
    y_dn = gated_deltanet(dn_qkv, dn_beta, dn_a, dn_gate, conv_w, a_log, dt_bias, out_norm)
    y = jnp.concatenate([y_sb, y_dn], axis=-1).astype(h.dtype)
    return y @ w_out


def _ssm_combine(e1, e2):
    a1r, a1i, b1r, b1i = e1
    a2r, a2i, b2r, b2i = e2
    return (a2r * a1r - a2i * a1i, a2r * a1i + a2i * a1r,
            a2r * b1r - a2i * b1i + b2r, a2r * b1i + a2i * b1r + b2i)


def s5_mixer(h, lam_re, lam_im, log_step, b_re, b_im, c_re, c_im, d_skip, w_glu, b_glu):
    bsz, l, _ = h.shape
    f32 = jnp.float32
    u = h.astype(f32).reshape(bsz, l, S5_GROUPS, S5_GROUP)
    lre = jnp.minimum(lam_re.astype(f32), S5_MIN_NEG)
    lim = lam_im.astype(f32)
    dt = jnp.exp(log_step.astype(f32))[:, None]
    mag = jnp.exp(lre * dt)
    ab_re, ab_im = mag * jnp.cos(lim * dt), mag * jnp.sin(lim * dt)
    den = lre * lre + lim * lim
    nr, ni = ab_re - 1.0, ab_im
    coef_re = (nr * lre + ni * lim) / den
    coef_im = (ni * lre - nr * lim) / den
    br, bi = b_re.astype(f32), b_im.astype(f32)
    bb_re = coef_re[..., None] * br - coef_im[..., None] * bi
    bb_im = coef_re[..., None] * bi + coef_im[..., None] * br
    cr, ci = c_re.astype(f32), c_im.astype(f32)
    a_shape = (bsz, S5_CHUNK, S5_GROUPS, S5_STATE)
    a_re_b = jnp.broadcast_to(ab_re, a_shape)
    a_im_b = jnp.broadcast_to(ab_im, a_shape)
    n = l // S5_CHUNK
    u_chunks = jnp.moveaxis(u.reshape(bsz, n, S5_CHUNK, S5_GROUPS, S5_GROUP), 1, 0)

    def chunk(carry, u_c):
        hr0, hi0 = carry
        bu_re = jnp.einsum('blgc,gpc->blgp', u_c, bb_re)
        bu_im = jnp.einsum('blgc,gpc->blgp', u_c, bb_im)
        pr, pi, lr, li = lax.associative_scan(_ssm_combine, (a_re_b, a_im_b, bu_re, bu_im), axis=1)
        hr = lr + pr * hr0[:, None] - pi * hi0[:, None]
        hi = li + pr * hi0[:, None] + pi * hr0[:, None]
        y = jnp.einsum('blgp,gcp->blgc', hr, cr) - jnp.einsum('blgp,gcp->blgc', hi, ci)
        return (hr[:, -1], hi[:, -1]), y

    h0 = jnp.zeros((bsz, S5_GROUPS, S5_STATE), f32)
    _, y = lax.scan(chunk, (h0, h0), u_chunks)
    u_flat = u.reshape(bsz, l, D_MODEL)
    y = jnp.moveaxis(y, 0, 1).reshape(bsz, l, D_MODEL) + d_skip.astype(f32) * u_flat
    y = jax.nn.gelu(y)
    zv, zg = jnp.split(y @ w_glu.astype(f32) + b_glu.astype(f32), 2, axis=-1)
    return (zv * jax.nn.sigmoid(zg)).astype(h.dtype)


def hier_moe(h, w_rg, w_re, w_e_in, w_e_out):
    b, l, d = h.shape
    f32 = jnp.float32
    xt = h.reshape(b * l, d)
    grp_logits = (xt @ w_rg).astype(f32)
    grp_prob = jax.nn.softmax(grp_logits, axis=-1)
    grp = jnp.argmax(grp_logits, axis=-1)
    grp_onehot = jax.nn.one_hot(grp, MOE_GROUPS, dtype=f32)
    grp_w = jnp.sum(grp_prob * grp_onehot, axis=-1)
    exp_logits = (xt @ w_re).astype(f32).reshape(-1, MOE_GROUPS, MOE_PER_GROUP)
    in_grp = jnp.take_along_axis(exp_logits, grp[:, None, None], axis=1)[:, 0]
    top_v, top_i = lax.top_k(in_grp, MOE_TOPK)
    top_w = jax.nn.softmax(top_v, axis=-1)
    within = jnp.einsum('tk,tke->te', top_w, jax.nn.one_hot(top_i, MOE_PER_GROUP, dtype=f32))
    gate = (grp_w[:, None, None] * grp_onehot[:, :, None] * within[:, None, :]).reshape(-1, N_EXPERTS)
    y = jnp.zeros((b * l, d), f32)
    for e in range(N_EXPERTS):
        a, v = jnp.split(xt @ w_e_in[e], 2, axis=-1)
        y = y + gate[:, e:e + 1] * ((jax.nn.silu(a) * v) @ w_e_out[e]).astype(f32)
    return y.reshape(b, l, d).astype(h.dtype)


def setup_inputs(seed: int = 0) -> dict:
    key = jax.random.key(seed)
    ks = jax.random.split(key, 32)
    nrm = lambda k, s, sc: jax.random.normal(k, s, jnp.float32) * sc
    gain = lambda k, s: 1.0 + 0.02 * jax.random.normal(k, s, jnp.float32)
    x = jax.random.normal(ks[0], (BATCH, SEQ, D_MODEL), jnp.float32)
    mix_norm_even = gain(ks[1], (N_EVEN, D_MODEL))
    w_in_even = nrm(ks[2], (N_EVEN, D_MODEL, IN_DIM_EVEN), D_MODEL ** -0.5)
    dn_conv_w = nrm(ks[3], (N_EVEN, DN_CONV, 3 * DN_WIDTH), DN_CONV ** -0.5)
    dn_a_log = jnp.log(jax.random.uniform(ks[4], (N_EVEN, DN_HEADS), jnp.float32, 1.0, 16.0))
    dt = jnp.exp(jax.random.uniform(ks[5], (N_EVEN, DN_HEADS), jnp.float32,
                                    math.log(1e-3), math.log(1e-1)))
    dn_dt_bias = dt + jnp.log(-jnp.expm1(-dt))
    dn_out_norm = gain(ks[6], (N_EVEN, DN_HEAD_DIM))
    w_out_even = nrm(ks[7], (N_EVEN, MIX_WIDTH_EVEN, D_MODEL), MIX_WIDTH_EVEN ** -0.5)
    mix_norm_odd = gain(ks[8], (N_ODD, D_MODEL))
    s5_lambda_re = -0.5 + 0.01 * jax.random.normal(ks[9], (N_ODD, S5_GROUPS, S5_STATE), jnp.float32)
    s5_lambda_im = (math.pi * jnp.arange(S5_STATE, dtype=jnp.float32)
                    + 0.01 * jax.random.normal(ks[10], (N_ODD, S5_GROUPS, S5_STATE), jnp.float32))
    s5_log_step = jax.random.uniform(ks[11], (N_ODD, S5_GROUPS), jnp.float32,
                                     math.log(1e-3), math.log(1e-1))
    s5_b_re = nrm(ks[12], (N_ODD, S5_GROUPS, S5_STATE, S5_GROUP), (2 * S5_GROUP) ** -0.5)
    s5_b_im = nrm(ks[13], (N_ODD, S5_GROUPS, S5_STATE, S5_GROUP), (2 * S5_GROUP) ** -0.5)
    s5_c_re = nrm(ks[14], (N_ODD, S5_GROUPS, S5_GROUP, S5_STATE), (2 * S5_STATE) ** -0.5)
    s5_c_im = nrm(ks[15], (N_ODD, S5_GROUPS, S5_GROUP, S5_STATE), (2 * S5_STATE) ** -0.5)
    s5_d = nrm(ks[16], (N_ODD, D_MODEL), 1.0)
    w_glu = nrm(ks[17], (N_ODD, D_MODEL, 2 * D_MODEL), D_MODEL ** -0.5)
    b_glu = nrm(ks[18], (N_ODD, 2 * D_MODEL), 0.01)
    ffn_norm = gain(ks[19], (DEPTH, D_MODEL))
    w_router_group = nrm(ks[20], (DEPTH, D_MODEL, MOE_GROUPS), D_MODEL ** -0.5)
    w_router_expert = nrm(ks[21], (DEPTH, D_MODEL, N_EXPERTS), D_MODEL ** -0.5)
    w_expert_in = nrm(ks[22], (DEPTH, N_EXPERTS, D_MODEL, 2 * EXPERT_FF), D_MODEL ** -0.5)
    w_expert_out = nrm(ks[23], (DEPTH, N_EXPERTS, EXPERT_FF, D_MODEL), EXPERT_FF ** -0.5)
    final_norm = gain(ks[24], (D_MODEL,))
    return {'x': x, 'mix_norm_even': mix_norm_even, 'w_in_even': w_in_even,
            'dn_conv_w': dn_conv_w, 'dn_a_log': dn_a_log, 'dn_dt_bias': dn_dt_bias,
            'dn_out_norm': dn_out_norm, 'w_out_even': w_out_even,
            'mix_norm_odd': mix_norm_odd, 's5_lambda_re': s5_lambda_re,
            's5_lambda_im': s5_lambda_im, 's5_log_step': s5_log_step,
            's5_b_re': s5_b_re, 's5_b_im': s5_b_im, 's5_c_re': s5_c_re, 's5_c_im': s5_c_im,
            's5_d': s5_d, 'w_glu': w_glu, 'b_glu': b_glu, 'ffn_norm': ffn_norm,
            'w_router_group': w_router_group, 'w_router_expert': w_router_expert,
            'w_expert_in': w_expert_in, 'w_expert_out': w_expert_out, 'final_norm': final_norm}


def reference(x, mix_norm_even, w_in_even, dn_conv_w, dn_a_log, dn_dt_bias, dn_out_norm,
              w_out_even, mix_norm_odd, s5_lambda_re, s5_lambda_im, s5_log_step,
              s5_b_re, s5_b_im, s5_c_re, s5_c_im, s5_d, w_glu, b_glu, ffn_norm,
              w_router_group, w_router_expert, w_expert_in, w_expert_out, final_norm):
    h = x
    for layer in range(DEPTH):
        j = layer // 2
        if layer % 2 == 0:
            h = h + even_mixer(rmsnorm(h, mix_norm_even[j]), w_in_even[j], dn_conv_w[j],
                               dn_a_log[j], dn_dt_bias[j], dn_out_norm[j], w_out_even[j])
        else:
            h = h + s5_mixer(rmsnorm(h, mix_norm_odd[j]), s5_lambda_re[j], s5_lambda_im[j],
                             s5_log_step[j], s5_b_re[j], s5_b_im[j], s5_c_re[j], s5_c_im[j],
                             s5_d[j], w_glu[j], b_glu[j])
        h = h + hier_moe(rmsnorm(h, ffn_norm[layer]), w_router_group[layer],
                         w_router_expert[layer], w_expert_in[layer], w_expert_out[layer])
    return rmsnorm(h, final_norm)
```

```python
import functools
import math

import jax
import jax.numpy as jnp
from jax import lax
from jax.experimental import pallas as pl
from jax.experimental.pallas import tpu as pltpu

F32 = jnp.float32
BF16 = jnp.bfloat16
HIGHEST = lax.Precision.HIGHEST

LANES = 128
SUBLANES = 8

D_MODEL = 1024
SB_HEADS = 8
SB_HEAD_DIM = 64
SB_WIDTH = SB_HEADS * SB_HEAD_DIM
DN_HEADS = 4
DN_HEAD_DIM = 128
DN_WIDTH = DN_HEADS * DN_HEAD_DIM
DN_CONV = 4
DN_CHUNK = 64
S5_GROUP = 16
S5_GROUPS = D_MODEL // S5_GROUP
S5_STATE = 64
S5_MIN_NEG = -1e-4
MOE_GROUPS = 4
MOE_PER_GROUP = 8
N_EXPERTS = MOE_GROUPS * MOE_PER_GROUP
EXPERT_FF = 256
RMS_EPS = 1e-6
L2_EPS = 1e-6

ROUTE_COLS = LANES
EXPERT_COL0 = MOE_GROUPS
ROUTE_OUT = 8

MOE_TILE = 256


def _softplus(x):
    return jnp.maximum(x, 0.0) + jnp.log1p(jnp.exp(-jnp.abs(x)))


def _sigmoid(x):
    return 1.0 / (1.0 + jnp.exp(-x))


def _rms(x, g):
    ms = jnp.mean(x * x, axis=-1, keepdims=True)
    return x * lax.rsqrt(ms + RMS_EPS) * g


def _bdot(a, b):
    return jnp.dot(a.astype(BF16), b.astype(BF16), preferred_element_type=F32)


def _bdot_nt(a, b):
    return lax.dot_general(a.astype(BF16), b.astype(BF16), (((1,), (1,)), ((), ())),
                           preferred_element_type=F32)


def _hdot(a, b):
    return jnp.dot(a, b, precision=HIGHEST, preferred_element_type=F32)


def _split2(x):
    hi = x.astype(BF16)
    lo = (x - hi.astype(F32)).astype(BF16)
    return hi, lo


def _inproj_kernel(x_ref, gn_ref, wsb_ref, wdn_ref, wgate_ref, wbg_ref, alog_ref, dtb_ref,
                   sb_ref, dn_ref, gate_ref, beta_ref, g_ref):
    xn = _rms(x_ref[...], gn_ref[...]).astype(BF16)
    sb_ref[...] = jnp.dot(xn, wsb_ref[...], preferred_element_type=F32).astype(BF16)
    dn_ref[...] = jnp.dot(xn, wdn_ref[...], preferred_element_type=F32)
    gate_ref[...] = jnp.dot(xn, wgate_ref[...], preferred_element_type=F32)
    bg = jnp.dot(xn, wbg_ref[...], preferred_element_type=F32)
    beta_ref[...] = _sigmoid(bg[:, :DN_WIDTH])
    g_ref[...] = -jnp.exp(alog_ref[...]) * _softplus(bg[:, DN_WIDTH:] + dtb_ref[...])


def _inproj(x2, gn, wsb, wdn, wgate, wbg, alog_v, dtb_v, tm=256):
    t, d = x2.shape
    full = lambda shp: pl.BlockSpec(shp, lambda i: (0, 0))
    row = lambda n: pl.BlockSpec((tm, n), lambda i: (i, 0))
    return pl.pallas_call(
        _inproj_kernel,
        grid=(t // tm,),
        in_specs=[row(d), full((1, d)), full(wsb.shape), full(wdn.shape), full(wgate.shape),
                  full(wbg.shape), full((1, DN_WIDTH)), full((1, DN_WIDTH))],
        out_specs=[row(3 * SB_WIDTH), row(3 * DN_WIDTH), row(DN_WIDTH), row(DN_WIDTH),
                   row(DN_WIDTH)],
        out_shape=[jax.ShapeDtypeStruct((t, 3 * SB_WIDTH), BF16),
                   jax.ShapeDtypeStruct((t, 3 * DN_WIDTH), F32),
                   jax.ShapeDtypeStruct((t, DN_WIDTH), F32),
                   jax.ShapeDtypeStruct((t, DN_WIDTH), F32),
                   jax.ShapeDtypeStruct((t, DN_WIDTH), F32)],
        compiler_params=pltpu.CompilerParams(dimension_semantics=("arbitrary",)),
        name="inproj",
    )(x2, gn, wsb, wdn, wgate, wbg, alog_v, dtb_v)


def _sb_kernel(q_ref, k_ref, v_ref, o_ref, *, blk):
    i = pl.program_id(2)
    row = lax.broadcasted_iota(jnp.int32, (blk, blk), 0)
    col = lax.broadcasted_iota(jnp.int32, (blk, blk), 1)
    later = (row > col).astype(BF16)
    causal = col < row
    lane = lax.broadcasted_iota(jnp.int32, (blk, LANES), 1)
    head_a = lane < SB_HEAD_DIM

    q = q_ref[0] * (1.0 / math.sqrt(SB_HEAD_DIM))
    zero = jnp.zeros_like(q)
    q_heads = (jnp.where(head_a, q, zero), jnp.where(head_a, zero, q))

    def block(j, carries, acc, masked):
        start = pl.multiple_of(j * blk, blk)
        kj = k_ref[0, pl.ds(start, blk), :]
        vj = v_ref[0, pl.ds(start, blk), :]
        zv = jnp.zeros_like(vj)
        v_heads = (jnp.where(head_a, vj, zv), jnp.where(head_a, zv, vj))
        new_carries = []
        for hh in range(2):
            z = lax.dot_general(q_heads[hh], kj, (((1,), (1,)), ((), ())),
                                preferred_element_type=F32)
            l1m = -_softplus(z)
            l1m_m = jnp.where(causal, l1m, 0.0) if masked else l1m
            hi, lo = _split2(l1m_m)
            after = (jnp.dot(hi, later, preferred_element_type=F32)
                     + jnp.dot(lo, later, preferred_element_type=F32) + carries[hh])
            p = jnp.exp(z + l1m + after)
            if masked:
                p = jnp.where(causal, p, 0.0)
            acc = acc + jnp.dot(p.astype(BF16), v_heads[hh], preferred_element_type=F32)
            new_carries.append(after[:, :1] + l1m_m[:, :1])
        return tuple(new_carries), acc

    c0 = jnp.zeros((blk, 1), F32)
    carries, acc = block(i, (c0, c0), jnp.zeros((blk, LANES), F32), True)

    def body(jj, state):
        cs, a = state
        return block(i - 1 - jj, cs, a, False)

    carries, acc = lax.fori_loop(0, i, body, (carries, acc))
    o_ref[0] = acc.astype(o_ref.dtype)


def _sb_attention(sb3, blk=256):
    b, l, _ = sb3.shape
    npair = SB_WIDTH // LANES
    return pl.pallas_call(
        functools.partial(_sb_kernel, blk=blk),
        grid=(b, npair, l // blk),
        in_specs=[pl.BlockSpec((1, blk, LANES), lambda bi, hp, i: (bi, i, hp)),
                  pl.BlockSpec((1, l, LANES), lambda bi, hp, i: (bi, 0, npair + hp)),
                  pl.BlockSpec((1, l, LANES), lambda bi, hp, i: (bi, 0, 2 * npair + hp))],
        out_specs=pl.BlockSpec((1, blk, LANES), lambda bi, hp, i: (bi, i, hp)),
        out_shape=jax.ShapeDtypeStruct((b, l, SB_WIDTH), BF16),
        compiler_params=pltpu.CompilerParams(
            dimension_semantics=("arbitrary", "arbitrary", "arbitrary")),
        name="sb_attention",
    )(sb3, sb3, sb3)


def _dnconv_kernel(x_ref, halo_ref, w_ref, o_ref, pad_ref, *, tt):
    t = pl.program_id(1)
    halo = halo_ref[0]
    pad_ref[pl.ds(0, SUBLANES), :] = jnp.where(t > 0, halo, jnp.zeros_like(halo))
    pad_ref[pl.ds(SUBLANES, tt), :] = x_ref[0]
    acc = None
    for k in range(DN_CONV):
        off = SUBLANES - (DN_CONV - 1) + k
        term = pad_ref[pl.ds(off, tt), :] * w_ref[pl.ds(k, 1), :]
        acc = term if acc is None else acc + term
    y = acc * _sigmoid(acc)
    for blk in range(3 * DN_HEADS):
        yb = y[:, blk * LANES:(blk + 1) * LANES]
        if blk < 2 * DN_HEADS:
            yb = yb * lax.rsqrt(jnp.sum(yb * yb, axis=-1, keepdims=True) + L2_EPS)
            if blk < DN_HEADS:
                yb = yb * (DN_HEAD_DIM ** -0.5)
        o_ref[0, :, blk * LANES:(blk + 1) * LANES] = yb


def _dn_conv(dn3, conv_w, tt=512):
    b, l, c = dn3.shape
    hb = tt // SUBLANES
    return pl.pallas_call(
        functools.partial(_dnconv_kernel, tt=tt),
        grid=(b, l // tt),
        in_specs=[pl.BlockSpec((1, tt, c), lambda bi, t: (bi, t, 0)),
                  pl.BlockSpec((1, SUBLANES, c), lambda bi, t: (bi, jnp.maximum(t * hb - 1, 0), 0)),
                  pl.BlockSpec((DN_CONV, c), lambda bi, t: (0, 0))],
        out_specs=pl.BlockSpec((1, tt, c), lambda bi, t: (bi, t, 0)),
        out_shape=jax.ShapeDtypeStruct((b, l, c), F32),
        scratch_shapes=[pltpu.VMEM((tt + SUBLANES, c), F32)],
        compiler_params=pltpu.CompilerParams(dimension_semantics=("arbitrary", "arbitrary")),
        name="dn_conv",
    )(dn3, dn3, conv_w)


def _delta_kernel(q_ref, k_ref, v_ref, beta_ref, g_ref, gate_ref, on_ref, o_ref,
                  w_s, u_s, attn_s, qd_s, ktt_s, cd_s, *, nchunk):
    c = DN_CHUNK
    row = lax.broadcasted_iota(jnp.int32, (c, c), 0)
    col = lax.broadcasted_iota(jnp.int32, (c, c), 1)
    incl = (row >= col)
    strict = (row > col)
    tri_incl = incl.astype(F32)
    eye = (row == col).astype(F32)

    def prep(n, _):
        r0 = pl.multiple_of(n * c, c)
        q = q_ref[0, pl.ds(r0, c), :]
        k = k_ref[0, pl.ds(r0, c), :]
        v = v_ref[0, pl.ds(r0, c), :]
        beta = beta_ref[0, pl.ds(r0, c), :]
        g = g_ref[0, pl.ds(r0, c), :]
        dec = _hdot(tri_incl, g)
        gw = jnp.where(strict, g[:, :c], 0.0)
        dd = _hdot(tri_incl, gw)
        dmask = jnp.where(incl, jnp.exp(dd), 0.0)
        kb = k * beta
        m = jnp.where(strict, _bdot_nt(kb, k) * dmask, 0.0)
        x = eye - m
        p = _hdot(m, m)
        for it in range(5):
            x = x + _hdot(x, p)
            if it < 4:
                p = _hdot(p, p)
        e = jnp.exp(dec)
        w_s[pl.ds(r0, c), :] = _hdot(x, kb * e)
        u_s[pl.ds(r0, c), :] = _hdot(x, v * beta)
        attn_s[pl.ds(r0, c), :] = _bdot_nt(q, k) * dmask
        qd_s[pl.ds(r0, c), :] = q * e
        dec_last = dec[c - 1:c, :]
        kt = k * jnp.exp(dec_last - dec)
        ktt_s[pl.ds(pl.multiple_of(n * DN_HEAD_DIM, DN_HEAD_DIM), DN_HEAD_DIM), :] = kt.T
        cd_s[pl.ds(n, 1), :] = jnp.exp(dec_last)
        return 0

    lax.fori_loop(0, nchunk, prep, 0)

    def step(n, state):
        r0 = pl.multiple_of(n * c, c)
        v_new = u_s[pl.ds(r0, c), :] - _bdot(w_s[pl.ds(r0, c), :], state)
        o = _bdot(qd_s[pl.ds(r0, c), :], state) + _bdot(attn_s[pl.ds(r0, c), :], v_new)
        ktt = ktt_s[pl.ds(pl.multiple_of(n * DN_HEAD_DIM, DN_HEAD_DIM), DN_HEAD_DIM), :]
        state = state * cd_s[pl.ds(n, 1), :] + _bdot(ktt, v_new)
        z = gate_ref[0, pl.ds(r0, c), :]
        y = _rms(o, on_ref[...]) * (z * _sigmoid(z))
        o_ref[0, pl.ds(r0, c), :] = y.astype(o_ref.dtype)
        return state

    lax.fori_loop(0, nchunk, step, jnp.zeros((DN_HEAD_DIM, DN_HEAD_DIM), F32))


def _delta_rule(qkv3, beta3, g3, gate3, out_norm):
    b, l, _ = qkv3.shape
    nchunk = l // DN_CHUNK
    seq = lambda off: pl.BlockSpec((1, l, LANES), lambda bi, h: (bi, 0, off + h))
    return pl.pallas_call(
        functools.partial(_delta_kernel, nchunk=nchunk),
        grid=(b, DN_HEADS),
        in_specs=[seq(0), seq(DN_HEADS), seq(2 * DN_HEADS), seq(0), seq(0), seq(0),
                  pl.BlockSpec((1, LANES), lambda bi, h: (0, 0))],
        out_specs=seq(0),
        out_shape=jax.ShapeDtypeStruct((b, l, DN_WIDTH), BF16),
        scratch_shapes=[pltpu.VMEM((l, LANES), F32), pltpu.VMEM((l, LANES), F32),
                        pltpu.VMEM((l, DN_CHUNK), F32), pltpu.VMEM((l, LANES), F32),
                        pltpu.VMEM((nchunk * DN_HEAD_DIM, DN_CHUNK), F32),
                        pltpu.VMEM((nchunk, LANES), F32)],
        compiler_params=pltpu.CompilerParams(dimension_semantics=("arbitrary", "arbitrary")),
        name="delta_rule",
    )(qkv3, qkv3, qkv3, beta3, g3, gate3, out_norm)


def _route(h, gff, wr, counts):
    tm = h.shape[0]
    xn = _rms(h, gff)
    logits = _hdot(xn, wr)
    col = lax.broadcasted_iota(jnp.int32, (tm, ROUTE_COLS), 1).astype(F32)
    neg = jnp.float32(-jnp.inf)
    first = lambda m: jnp.min(jnp.where(m, col, float(ROUTE_COLS)), axis=-1, keepdims=True)

    gl = jnp.where(col < MOE_GROUPS, logits, neg)
    gmax = jnp.max(gl, axis=-1, keepdims=True)
    gidx = first(gl == gmax)
    grp_w = 1.0 / jnp.sum(jnp.exp(gl - gmax), axis=-1, keepdims=True)

    lo = EXPERT_COL0 + MOE_PER_GROUP * gidx
    el = jnp.where((col >= lo) & (col < lo + MOE_PER_GROUP), logits, neg)
    v1 = jnp.max(el, axis=-1, keepdims=True)
    i1 = first(el == v1)
    el2 = jnp.where(col == i1, neg, el)
    v2 = jnp.max(el2, axis=-1, keepdims=True)
    i2 = first(el2 == v2)
    ex = jnp.exp(v2 - v1)
    w1 = 1.0 / (1.0 + ex)
    w2 = ex / (1.0 + ex)

    pick1 = col == i1
    pick2 = col == i2
    onehot = (pick1 | pick2).astype(BF16)
    r = lax.broadcasted_iota(jnp.int32, (tm, tm), 0)
    c = lax.broadcasted_iota(jnp.int32, (tm, tm), 1)
    before = (c < r).astype(BF16)
    prior = jnp.dot(before, onehot, preferred_element_type=F32) + counts
    rank1 = jnp.sum(jnp.where(pick1, prior, 0.0), axis=-1, keepdims=True)
    rank2 = jnp.sum(jnp.where(pick2, prior, 0.0), axis=-1, keepdims=True)
    counts = counts + jnp.sum(onehot.astype(F32), axis=0, keepdims=True)

    oc = lax.broadcasted_iota(jnp.int32, (tm, ROUTE_OUT), 1)
    e1 = i1 - float(EXPERT_COL0)
    e2 = i2 - float(EXPERT_COL0)
    vals = (e1, e2, rank1, rank2, grp_w * w1, grp_w * w2)
    info = jnp.zeros((tm, ROUTE_OUT), F32)
    for idx, val in enumerate(vals):
        info = jnp.where(oc == idx, val, info)
    return xn, info, counts


def _outproj_kernel(ysb_ref, ydn_ref, x_ref, wa_ref, wb_ref, gff_ref, wr_ref,
                    h_ref, xn_ref, info_ref, cnt_ref, cnt_s):
    @pl.when(pl.program_id(0) == 0)
    def _():
        cnt_s[...] = jnp.zeros_like(cnt_s)

    h = (x_ref[...] + jnp.dot(ysb_ref[...], wa_ref[...], preferred_element_type=F32)
         + jnp.dot(ydn_ref[...], wb_ref[...], preferred_element_type=F32))
    h_ref[...] = h
    xn, info, counts = _route(h, gff_ref[...], wr_ref[...], cnt_s[...])
    xn_ref[...] = xn
    info_ref[...] = info
    cnt_s[...] = counts
    cnt_ref[...] = counts


def _route_outs(t, d, tm):
    row = lambda n: pl.BlockSpec((tm, n), lambda i: (i, 0))
    specs = [row(d), row(d), row(ROUTE_OUT), pl.BlockSpec((1, ROUTE_COLS), lambda i: (0, 0))]
    shapes = [jax.ShapeDtypeStruct((t, d), F32), jax.ShapeDtypeStruct((t, d), F32),
              jax.ShapeDtypeStruct((t, ROUTE_OUT), F32),
              jax.ShapeDtypeStruct((1, ROUTE_COLS), F32)]
    return specs, shapes


def _outproj(ysb, ydn, x2, wa, wb, gff, wr, tm=256):
    t, d = x2.shape
    full = lambda a: pl.BlockSpec(a.shape, lambda i: (0, 0))
    row = lambda n: pl.BlockSpec((tm, n), lambda i: (i, 0))
    out_specs, out_shape = _route_outs(t, d, tm)
    return pl.pallas_call(
        _outproj_kernel,
        grid=(t // tm,),
        in_specs=[row(SB_WIDTH), row(DN_WIDTH), row(d), full(wa), full(wb), full(gff), full(wr)],
        out_specs=out_specs, out_shape=out_shape,
        scratch_shapes=[pltpu.VMEM((1, ROUTE_COLS), F32)],
        compiler_params=pltpu.CompilerParams(dimension_semantics=("arbitrary",)),
        name="outproj_route",
    )(ysb, ydn, x2, wa, wb, gff, wr)


def _dispatch_kernel(pos_ref, x_ref, xs_in, xs_out, sem, *, tm):
    del xs_in

    def copy(r, p):
        return pltpu.make_async_copy(x_ref.at[pl.ds(r, 1)], xs_out.at[pl.ds(p, 1)], sem)

    def issue(r, _):
        copy(r, pos_ref[0, 0, 2 * r]).start()
        copy(r, pos_ref[0, 0, 2 * r + 1]).start()
        return 0

    lax.fori_loop(0, tm, issue, 0)

    def drain(r, _):
        copy(0, 0).wait()
        copy(0, 0).wait()
        return 0

    lax.fori_loop(0, tm, drain, 0)


def _dispatch(xn, pos, n_rows, tm=256):
    t, d = xn.shape
    pos3 = pos.reshape(t // tm, 1, 2 * tm)
    xs0 = jnp.zeros((n_rows, d), F32)
    return pl.pallas_call(
        functools.partial(_dispatch_kernel, tm=tm),
        grid=(t // tm,),
        in_specs=[pl.BlockSpec((1, 1, 2 * tm), lambda i: (i, 0, 0), memory_space=pltpu.SMEM),
                  pl.BlockSpec((tm, d), lambda i: (i, 0)),
                  pl.BlockSpec(memory_space=pl.ANY)],
        out_specs=pl.BlockSpec(memory_space=pl.ANY),
        out_shape=jax.ShapeDtypeStruct((n_rows, d), F32),
        scratch_shapes=[pltpu.SemaphoreType.DMA(())],
        input_output_aliases={2: 0},
        compiler_params=pltpu.CompilerParams(dimension_semantics=("arbitrary",)),
        name="moe_dispatch",
    )(pos3, xn, xs0)


def _gmm_kernel(te_ref, tv_ref, xs_ref, win_ref, wout_ref, ys_ref):
    i = pl.program_id(0)

    @pl.when(tv_ref[i] > 0)
    def _():
        h = jnp.dot(xs_ref[...].astype(BF16), win_ref[0], preferred_element_type=F32)
        a = h[:, :EXPERT_FF]
        v = h[:, EXPERT_FF:]
        act = (a * _sigmoid(a)) * v
        ys_ref[...] = jnp.dot(act.astype(BF16), wout_ref[0], preferred_element_type=F32)

    @pl.when(tv_ref[i] == 0)
    def _():
        ys_ref[...] = jnp.zeros_like(ys_ref)


def _gmm(xs, tile_expert, tile_valid, w_in, w_out, tm=MOE_TILE):
    n, d = xs.shape
    return pl.pallas_call(
        _gmm_kernel,
        grid_spec=pltpu.PrefetchScalarGridSpec(
            num_scalar_prefetch=2,
            grid=(n // tm,),
            in_specs=[pl.BlockSpec((tm, d), lambda i, te, tv: (i, 0)),
                      pl.BlockSpec((1, d, 2 * EXPERT_FF), lambda i, te, tv: (te[i], 0, 0)),
                      pl.BlockSpec((1, EXPERT_FF, d), lambda i, te, tv: (te[i], 0, 0))],
            out_specs=pl.BlockSpec((tm, d), lambda i, te, tv: (i, 0))),
        out_shape=jax.ShapeDtypeStruct((n, d), F32),
        compiler_params=pltpu.CompilerParams(dimension_semantics=("arbitrary",)),
        name="moe_gmm",
    )(tile_expert, tile_valid, xs, w_in, w_out)


def _combine_kernel(pos_ref, h_ref, gates_ref, gn_ref, ys_hbm, out_ref, xn_ref, buf, sem,
                    *, tm):
    def copy(r, p, slot):
        return pltpu.make_async_copy(ys_hbm.at[pl.ds(p, 1)], buf.at[slot, pl.ds(r, 1)], sem)

    def issue(r, _):
        copy(r, pos_ref[0, 0, 2 * r], 0).start()
        copy(r, pos_ref[0, 0, 2 * r + 1], 1).start()
        return 0

    lax.fori_loop(0, tm, issue, 0)

    def drain(r, _):
        copy(0, 0, 0).wait()
        copy(0, 0, 1).wait()
        return 0

    lax.fori_loop(0, tm, drain, 0)
    gates = gates_ref[...]
    h = h_ref[...] + gates[:, 0:1] * buf[0] + gates[:, 1:2] * buf[1]
    out_ref[...] = h
    xn_ref[...] = _rms(h, gn_ref[...])


def _combine(h, pos, gates, ys, gnext, tm=256):
    t, d = h.shape
    pos3 = pos.reshape(t // tm, 1, 2 * tm)
    return pl.pallas_call(
        functools.partial(_combine_kernel, tm=tm),
        grid=(t // tm,),
        in_specs=[pl.BlockSpec((1, 1, 2 * tm), lambda i: (i, 0, 0), memory_space=pltpu.SMEM),
                  pl.BlockSpec((tm, d), lambda i: (i, 0)),
                  pl.BlockSpec((tm, 2), lambda i: (i, 0)),
                  pl.BlockSpec((1, d), lambda i: (0, 0)),
                  pl.BlockSpec(memory_space=pl.ANY)],
        out_specs=[pl.BlockSpec((tm, d), lambda i: (i, 0)), pl.BlockSpec((tm, d), lambda i: (i, 0))],
        out_shape=[jax.ShapeDtypeStruct((t, d), F32), jax.ShapeDtypeStruct((t, d), F32)],
        scratch_shapes=[pltpu.VMEM((2, tm, d), F32), pltpu.SemaphoreType.DMA(())],
        compiler_params=pltpu.CompilerParams(dimension_semantics=("arbitrary",)),
        name="moe_combine",
    )(pos3, h, gates, gnext, ys)


def _moe_plan(info, counts, tm=MOE_TILE):
    t = info.shape[0]
    n_rows = 2 * t + N_EXPERTS * tm
    cnt = counts[0, EXPERT_COL0:EXPERT_COL0 + N_EXPERTS].astype(jnp.int32)
    padded = ((cnt + tm - 1) // tm) * tm
    ends = jnp.cumsum(padded)
    offs = ends - padded
    e = info[:, 0:2].astype(jnp.int32)
    rank = info[:, 2:4].astype(jnp.int32)
    pos = offs[e] + rank
    tile_start = jnp.arange(n_rows // tm, dtype=jnp.int32) * tm
    te = jnp.sum((tile_start[:, None] >= ends[None, :]).astype(jnp.int32), axis=1)
    tv = (te < N_EXPERTS).astype(jnp.int32)
    te = jnp.minimum(te, N_EXPERTS - 1)
    return pos, info[:, 4:6], te, tv, n_rows


def _moe(h, xn, info, counts, w_in, w_out, gnext, perm=None):
    pos, gates, te, tv, n_rows = _moe_plan(info, counts)
    xs = _dispatch(xn, pos, n_rows)
    ys = _gmm(xs, te, tv, w_in, w_out)
    if perm is not None:
        h, pos, gates = perm(h), perm(pos), perm(gates)
    return _combine(h, pos, gates, ys, gnext)


def _s5_kernel(u_ref, bmat_ref, cmat_ref, a_ref, dskip_ref, y_ref, h_s, st_s, *, nt, nb):
    tc = pl.program_id(1)
    half = h_s.shape[1] // 2

    @pl.when(tc == 0)
    def _():
        st_s[...] = jnp.zeros_like(st_s)

    u = u_ref[...]
    h_s[...] = jnp.dot(u.astype(BF16), bmat_ref[0], preferred_element_type=F32)
    a_re = jnp.broadcast_to(a_ref[0, 0:1, :], (nb, half))
    a_im = jnp.broadcast_to(a_ref[0, 1:2, :], (nb, half))

    def step(t, carry):
        hr, hi = carry
        r0 = pl.multiple_of(t * nb, nb)
        br = h_s[pl.ds(r0, nb), pl.ds(0, half)]
        bi = h_s[pl.ds(r0, nb), pl.ds(half, half)]
        nr = a_re * hr - a_im * hi + br
        ni = a_re * hi + a_im * hr + bi
        h_s[pl.ds(r0, nb), pl.ds(0, half)] = nr
        h_s[pl.ds(r0, nb), pl.ds(half, half)] = ni
        return nr, ni

    hr, hi = lax.fori_loop(0, nt, step, (st_s[0], st_s[1]), unroll=8)
    st_s[0] = hr
    st_s[1] = hi
    y = jnp.dot(h_s[...].astype(BF16), cmat_ref[0], preferred_element_type=F32)
    y = y + dskip_ref[0] * u
    y = 0.5 * y * (1.0 + jnp.tanh(math.sqrt(2.0 / math.pi) * (y + 0.044715 * (y * y * y))))
    y_ref[...] = y.astype(y_ref.dtype)


def _s5(u2, bmat, cmat, amat, dskip3, nb, nt=64):
    t, d = u2.shape
    ncc = d // LANES
    width = bmat.shape[2]
    rows = nt * nb
    return pl.pallas_call(
        functools.partial(_s5_kernel, nt=nt, nb=nb),
        grid=(ncc, t // rows),
        in_specs=[pl.BlockSpec((rows, LANES), lambda cc, tc: (tc, cc)),
                  pl.BlockSpec((1, LANES, width), lambda cc, tc: (cc, 0, 0)),
                  pl.BlockSpec((1, width, LANES), lambda cc, tc: (cc, 0, 0)),
                  pl.BlockSpec((1, 2, width // 2), lambda cc, tc: (cc, 0, 0)),
                  pl.BlockSpec((1, 1, LANES), lambda cc, tc: (cc, 0, 0))],
        out_specs=pl.BlockSpec((rows, LANES), lambda cc, tc: (tc, cc)),
        out_shape=jax.ShapeDtypeStruct((t, d), BF16),
        scratch_shapes=[pltpu.VMEM((rows, width), F32), pltpu.VMEM((2, nb, width // 2), F32)],
        compiler_params=pltpu.CompilerParams(dimension_semantics=("arbitrary", "arbitrary")),
        name="s5_scan",
    )(u2, bmat, cmat, amat, dskip3)


def _glu_kernel(y_ref, h_ref, wv_ref, wg_ref, bv_ref, bg_ref, gff_ref, wr_ref,
                hn_ref, xn_ref, info_ref, cnt_ref, cnt_s):
    @pl.when(pl.program_id(0) == 0)
    def _():
        cnt_s[...] = jnp.zeros_like(cnt_s)

    y = y_ref[...]
    zv = jnp.dot(y, wv_ref[...], preferred_element_type=F32) + bv_ref[...]
    zg = jnp.dot(y, wg_ref[...], preferred_element_type=F32) + bg_ref[...]
    h = h_ref[...] + zv * _sigmoid(zg)
    hn_ref[...] = h
    xn, info, counts = _route(h, gff_ref[...], wr_ref[...], cnt_s[...])
    xn_ref[...] = xn
    info_ref[...] = info
    cnt_s[...] = counts
    cnt_ref[...] = counts


def _glu(y, h, wv, wg, bv, bg, gff, wr, tm=256):
    t, d = h.shape
    full = lambda a: pl.BlockSpec(a.shape, lambda i: (0, 0))
    row = lambda n: pl.BlockSpec((tm, n), lambda i: (i, 0))
    out_specs, out_shape = _route_outs(t, d, tm)
    return pl.pallas_call(
        _glu_kernel,
        grid=(t // tm,),
        in_specs=[row(d), row(d), full(wv), full(wg), full(bv), full(bg), full(gff), full(wr)],
        out_specs=out_specs, out_shape=out_shape,
        scratch_shapes=[pltpu.VMEM((1, ROUTE_COLS), F32)],
        compiler_params=pltpu.CompilerParams(dimension_semantics=("arbitrary",)),
        name="glu_route",
    )(y, h, wv, wg, bv, bg, gff, wr)


def _router_weights(w_rg, w_re):
    d = w_rg.shape[0]
    pad = jnp.zeros((d, ROUTE_COLS - MOE_GROUPS - N_EXPERTS), F32)
    return jnp.concatenate([w_rg, w_re, pad], axis=1)


def _s5_params(lam_re, lam_im, log_step, b_re, b_im, c_re, c_im):
    lre = jnp.minimum(lam_re, S5_MIN_NEG)
    lim = lam_im
    dt = jnp.exp(log_step)[:, None]
    mag = jnp.exp(lre * dt)
    ab_re, ab_im = mag * jnp.cos(lim * dt), mag * jnp.sin(lim * dt)
    den = lre * lre + lim * lim
    nr, ni = ab_re - 1.0, ab_im
    coef_re = (nr * lre + ni * lim) / den
    coef_im = (ni * lre - nr * lim) / den
    bb_re = coef_re[..., None] * b_re - coef_im[..., None] * b_im
    bb_im = coef_re[..., None] * b_im + coef_im[..., None] * b_re
    gpc = LANES // S5_GROUP
    ncc = S5_GROUPS // gpc
    eye = jnp.eye(gpc, dtype=F32)

    def in_block(bb):
        x = bb.reshape(ncc, gpc, S5_STATE, S5_GROUP)
        x = jnp.einsum('ngpc,gh->ngchp', x, eye)
        return x.reshape(ncc, LANES, gpc * S5_STATE)

    def out_block(cc):
        x = cc.reshape(ncc, gpc, S5_GROUP, S5_STATE)
        x = jnp.einsum('ngcp,gh->ngphc', x, eye)
        return x.reshape(ncc, gpc * S5_STATE, LANES)

    bmat = jnp.concatenate([in_block(bb_re), in_block(bb_im)], axis=2).astype(BF16)
    cmat = jnp.concatenate([out_block(c_re), out_block(-c_im)], axis=1).astype(BF16)
    amat = jnp.stack([ab_re.reshape(ncc, gpc * S5_STATE), ab_im.reshape(ncc, gpc * S5_STATE)],
                     axis=1)
    return bmat, cmat, amat


def kernel(x, mix_norm_even, w_in_even, dn_conv_w, dn_a_log, dn_dt_bias, dn_out_norm, w_out_even, mix_norm_odd, s5_lambda_re, s5_lambda_im, s5_log_step, s5_b_re, s5_b_im, s5_c_re, s5_c_im, s5_d, w_glu, b_glu, ffn_norm, w_router_group, w_router_expert, w_expert_in, w_expert_out, final_norm):
    b, l, d = x.shape
    t = b * l
    x2 = x.reshape(t, d)

    w_in = w_in_even[0]
    c_dn = 3 * SB_WIDTH
    c_beta = c_dn + 3 * DN_WIDTH
    c_a = c_beta + DN_HEADS
    c_gate = c_a + DN_HEADS
    rep = lambda w: jnp.repeat(w, DN_HEAD_DIM, axis=1)
    wbg = jnp.concatenate([rep(w_in[:, c_beta:c_a]), rep(w_in[:, c_a:c_gate])], axis=1)
    sb, dn, gate, beta_b, g_b = _inproj(
        x2, mix_norm_even[0][None], w_in[:, :c_dn].astype(BF16),
        w_in[:, c_dn:c_beta].astype(BF16), w_in[:, c_gate:].astype(BF16), wbg.astype(BF16),
        jnp.repeat(dn_a_log[0], DN_HEAD_DIM)[None], jnp.repeat(dn_dt_bias[0], DN_HEAD_DIM)[None])
    y_sb = _sb_attention(sb.reshape(b, l, 3 * SB_WIDTH))
    qkv = _dn_conv(dn.reshape(b, l, 3 * DN_WIDTH), dn_conv_w[0])
    y_dn = _delta_rule(qkv, beta_b.reshape(b, l, DN_WIDTH), g_b.reshape(b, l, DN_WIDTH),
                       gate.reshape(b, l, DN_WIDTH), dn_out_norm[0][None])
    w_out = w_out_even[0].astype(BF16)
    h1, xn, info, counts = _outproj(
        y_sb.reshape(t, SB_WIDTH), y_dn.reshape(t, DN_WIDTH), x2, w_out[:SB_WIDTH],
        w_out[SB_WIDTH:], ffn_norm[0][None], _router_weights(w_router_group[0], w_router_expert[0]))

    to_time_major = lambda a: a.reshape(b, l, -1).transpose(1, 0, 2).reshape(t, -1)
    h2, un = _moe(h1, xn, info, counts, w_expert_in[0].astype(BF16), w_expert_out[0].astype(BF16),
                  mix_norm_odd[0][None], perm=to_time_major)

    bmat, cmat, amat = _s5_params(s5_lambda_re[0], s5_lambda_im[0], s5_log_step[0], s5_b_re[0],
                                  s5_b_im[0], s5_c_re[0], s5_c_im[0])
    y = _s5(un, bmat, cmat, amat, s5_d[0].reshape(d // LANES, 1, LANES), nb=b)
    wg = w_glu[0].astype(BF16)
    h3, xn, info, counts = _glu(
        y, h2, wg[:, :d], wg[:, d:], b_glu[0][None, :d], b_glu[0][None, d:], ffn_norm[1][None],
        _router_weights(w_router_group[1], w_router_expert[1]))
    _, out = _moe(h3, xn, info, counts, w_expert_in[1].astype(BF16), w_expert_out[1].astype(BF16),
                  final_norm[None])
    return out.reshape(l, b, d).transpose(1, 0, 2)
```

```python
import functools
import math

import jax
import jax.numpy as jnp
from jax import lax
from jax.experimental import pallas as pl
from jax.experimental.pallas import tpu as pltpu

F32 = jnp.float32
BF16 = jnp.bfloat16
HIGHEST = lax.Precision.HIGHEST

LANES = 128
SUBLANES = 8

D_MODEL = 1024
SB_HEADS = 8
SB_HEAD_DIM = 64
SB_WIDTH = SB_HEADS * SB_HEAD_DIM
DN_HEADS = 4
DN_HEAD_DIM = 128
DN_WIDTH = DN_HEADS * DN_HEAD_DIM
DN_CONV = 4
DN_CHUNK = 64
DN_PREP_UNROLL = 8
S5_GROUP = 16
S5_GROUPS = D_MODEL // S5_GROUP
S5_STATE = 64
S5_MIN_NEG = -1e-4
MOE_GROUPS = 4
MOE_PER_GROUP = 8
N_EXPERTS = MOE_GROUPS * MOE_PER_GROUP
EXPERT_FF = 256
RMS_EPS = 1e-6
L2_EPS = 1e-6

ROUTE_COLS = LANES
EXPERT_COL0 = MOE_GROUPS
ROUTE_OUT = 8

MOE_TILE = 256
DMA_ISSUE_UNROLL = 8


def _softplus(x):
    return jnp.maximum(x, 0.0) + jnp.log1p(jnp.exp(-jnp.abs(x)))


def _sigmoid(x):
    return 1.0 / (1.0 + jnp.exp(-x))


def _rms(x, g):
    ms = jnp.mean(x * x, axis=-1, keepdims=True)
    return x * lax.rsqrt(ms + RMS_EPS) * g


def _bdot(a, b):
    return jnp.dot(a.astype(BF16), b.astype(BF16), preferred_element_type=F32)


def _bdot_nt(a, b):
    return lax.dot_general(a.astype(BF16), b.astype(BF16), (((1,), (1,)), ((), ())),
                           preferred_element_type=F32)


def _hdot(a, b):
    return jnp.dot(a, b, precision=HIGHEST, preferred_element_type=F32)


def _split2(x):
    hi = x.astype(BF16)
    lo = (x - hi.astype(F32)).astype(BF16)
    return hi, lo


def _dot3(a, b):
    dot = lambda u, v: jnp.dot(u, v, preferred_element_type=F32)
    return dot(a[0], b[0]) + dot(a[0], b[1]) + dot(a[1], b[0])


def _dot_exact_lhs(a_bf16, b):
    hi, lo = _split2(b)
    return (jnp.dot(a_bf16, hi, preferred_element_type=F32)
            + jnp.dot(a_bf16, lo, preferred_element_type=F32))


def _inproj_kernel(x_ref, gn_ref, wsb_ref, wdn_ref, wgate_ref, wbg_ref, alog_ref, dtb_ref,
                   sb_ref, dn_ref, gate_ref, beta_ref, g_ref):
    xn = _rms(x_ref[...], gn_ref[...]).astype(BF16)
    sb_ref[...] = jnp.dot(xn, wsb_ref[...], preferred_element_type=F32).astype(BF16)
    dn_ref[...] = jnp.dot(xn, wdn_ref[...], preferred_element_type=F32)
    gate_ref[...] = jnp.dot(xn, wgate_ref[...], preferred_element_type=F32)
    bg = jnp.dot(xn, wbg_ref[...], preferred_element_type=F32)
    beta_ref[...] = _sigmoid(bg[:, :DN_WIDTH])
    g_ref[...] = -jnp.exp(alog_ref[...]) * _softplus(bg[:, DN_WIDTH:] + dtb_ref[...])


def _inproj(x2, gn, wsb, wdn, wgate, wbg, alog_v, dtb_v, tm=256):
    t, d = x2.shape
    full = lambda shp: pl.BlockSpec(shp, lambda i: (0, 0))
    row = lambda n: pl.BlockSpec((tm, n), lambda i: (i, 0))
    return pl.pallas_call(
        _inproj_kernel,
        grid=(t // tm,),
        in_specs=[row(d), full((1, d)), full(wsb.shape), full(wdn.shape), full(wgate.shape),
                  full(wbg.shape), full((1, DN_WIDTH)), full((1, DN_WIDTH))],
        out_specs=[row(3 * SB_WIDTH), row(3 * DN_WIDTH), row(DN_WIDTH), row(DN_WIDTH),
                   row(DN_WIDTH)],
        out_shape=[jax.ShapeDtypeStruct((t, 3 * SB_WIDTH), BF16),
                   jax.ShapeDtypeStruct((t, 3 * DN_WIDTH), F32),
                   jax.ShapeDtypeStruct((t, DN_WIDTH), F32),
                   jax.ShapeDtypeStruct((t, DN_WIDTH), F32),
                   jax.ShapeDtypeStruct((t, DN_WIDTH), F32)],
        compiler_params=pltpu.CompilerParams(dimension_semantics=("arbitrary",)),
        name="inproj",
    )(x2, gn, wsb, wdn, wgate, wbg, alog_v, dtb_v)


def _sb_kernel(q_ref, k_ref, v_ref, o_ref, *, bq, bk, strip):
    i = pl.program_id(2)
    ndiag = bq // bk
    nstrip = bq // strip
    krow = lax.broadcasted_iota(jnp.int32, (bk, bk), 0)
    kcol = lax.broadcasted_iota(jnp.int32, (bk, bk), 1)
    neg_later = -((krow >= kcol).astype(BF16))
    srow = lax.broadcasted_iota(jnp.int32, (strip, bk), 0)
    scol = lax.broadcasted_iota(jnp.int32, (strip, bk), 1)
    head_a = lax.broadcasted_iota(jnp.int32, (1, LANES), 1) < SB_HEAD_DIM

    q = q_ref[0] * (1.0 / math.sqrt(SB_HEAD_DIM))
    zero = jnp.zeros_like(q)
    q_heads = (jnp.where(head_a, q, zero), jnp.where(head_a, zero, q))
    streams = [(hh, s) for s in range(nstrip) for hh in range(2)]
    q_st = [q_heads[hh][s * strip:(s + 1) * strip] for hh, s in streams]
    each = lambda f, *ls: [f(*a) for a in zip(*ls)]

    def keys(ref, j):
        return ref[0, pl.ds(pl.multiple_of(j * bk, bk), bk), :]

    def qk(qs, kj):
        return lax.dot_general(qs, kj, (((1,), (1,)), ((), ())), preferred_element_type=F32)

    def block(j, carries, accs, diag):
        mask_if = lambda x, cm: x if cm is None else jnp.where(cm, x, 0.0)
        if diag is None:
            act = list(range(len(streams)))
            masks = [None] * len(act)
        else:
            act = [n for n, (_, s) in enumerate(streams) if (s + 1) * strip > diag * bk]
            masks = [None if streams[n][1] * strip >= (diag + 1) * bk else
                     (scol + diag * bk) < (srow + streams[n][1] * strip) for n in act]
        kj = keys(k_ref, j)
        vj = keys(v_ref, j)
        zv = jnp.zeros_like(vj)
        v_heads = (jnp.where(head_a, vj, zv), jnp.where(head_a, zv, vj))
        qa = [q_st[n] for n in act]
        z = [qk(qs, kj) for qs in qa]
        sp = [jnp.maximum(x, 0.0) + jnp.log(1.0 + jnp.exp(-jnp.abs(x))) for x in z]
        sp = each(mask_if, sp, masks)
        split = each(_split2, sp)
        logw = each(lambda zz, hl: zz
                    + (jnp.dot(hl[0], neg_later, preferred_element_type=F32)
                       + jnp.dot(hl[1], neg_later, preferred_element_type=F32)), z, split)
        p = each(lambda lw, n: jnp.exp(lw + carries[n]), logw, act)
        p = each(mask_if, p, masks)
        carries = list(carries)
        accs = list(accs)
        for x, lw, zz, n in zip(p, logw, z, act):
            hh, s = streams[n]
            accs[s] = accs[s] + jnp.dot(x.astype(BF16), v_heads[hh], preferred_element_type=F32)
            carries[n] = carries[n] + (lw[:, :1] - zz[:, :1])
        return tuple(carries), tuple(accs)

    state = (tuple(jnp.zeros((strip, 1), F32) for _ in streams),
             tuple(jnp.zeros((strip, LANES), F32) for _ in range(nstrip)))
    for d in reversed(range(ndiag)):
        state = block(i * ndiag + d, state[0], state[1], d)
    nfull = i * ndiag
    carries, accs = lax.fori_loop(
        0, nfull, lambda jj, st: block(nfull - 1 - jj, st[0], st[1], None), state)
    for s in range(nstrip):
        o_ref[0, s * strip:(s + 1) * strip, :] = accs[s].astype(o_ref.dtype)


def _sb_attention(sb3, bq=1024, bk=256, strip=128):
    b, l, _ = sb3.shape
    npair = SB_WIDTH // LANES
    return pl.pallas_call(
        functools.partial(_sb_kernel, bq=bq, bk=bk, strip=strip),
        grid=(b, npair, l // bq),
        in_specs=[pl.BlockSpec((1, bq, LANES), lambda bi, hp, i: (bi, i, hp)),
                  pl.BlockSpec((1, l, LANES), lambda bi, hp, i: (bi, 0, npair + hp)),
                  pl.BlockSpec((1, l, LANES), lambda bi, hp, i: (bi, 0, 2 * npair + hp))],
        out_specs=pl.BlockSpec((1, bq, LANES), lambda bi, hp, i: (bi, i, hp)),
        out_shape=jax.ShapeDtypeStruct((b, l, SB_WIDTH), BF16),
        compiler_params=pltpu.CompilerParams(
            dimension_semantics=("arbitrary", "arbitrary", "arbitrary")),
        name="sb_attention",
    )(sb3, sb3, sb3)


def _dnconv_kernel(x_ref, halo_ref, w_ref, o_ref, pad_ref, *, tt):
    t = pl.program_id(1)
    halo = halo_ref[0]
    pad_ref[pl.ds(0, SUBLANES), :] = jnp.where(t > 0, halo, jnp.zeros_like(halo))
    pad_ref[pl.ds(SUBLANES, tt), :] = x_ref[0]
    acc = None
    for k in range(DN_CONV):
        off = SUBLANES - (DN_CONV - 1) + k
        term = pad_ref[pl.ds(off, tt), :] * w_ref[pl.ds(k, 1), :]
        acc = term if acc is None else acc + term
    y = acc * _sigmoid(acc)
    for blk in range(3 * DN_HEADS):
        yb = y[:, blk * LANES:(blk + 1) * LANES]
        if blk < 2 * DN_HEADS:
            yb = yb * lax.rsqrt(jnp.sum(yb * yb, axis=-1, keepdims=True) + L2_EPS)
            if blk < DN_HEADS:
                yb = yb * (DN_HEAD_DIM ** -0.5)
        o_ref[0, :, blk * LANES:(blk + 1) * LANES] = yb


def _dn_conv(dn3, conv_w, tt=512):
    b, l, c = dn3.shape
    hb = tt // SUBLANES
    return pl.pallas_call(
        functools.partial(_dnconv_kernel, tt=tt),
        grid=(b, l // tt),
        in_specs=[pl.BlockSpec((1, tt, c), lambda bi, t: (bi, t, 0)),
                  pl.BlockSpec((1, SUBLANES, c), lambda bi, t: (bi, jnp.maximum(t * hb - 1, 0), 0)),
                  pl.BlockSpec((DN_CONV, c), lambda bi, t: (0, 0))],
        out_specs=pl.BlockSpec((1, tt, c), lambda bi, t: (bi, t, 0)),
        out_shape=jax.ShapeDtypeStruct((b, l, c), F32),
        scratch_shapes=[pltpu.VMEM((tt + SUBLANES, c), F32)],
        compiler_params=pltpu.CompilerParams(dimension_semantics=("arbitrary", "arbitrary")),
        name="dn_conv",
    )(dn3, dn3, conv_w)


def _delta_kernel(q_ref, k_ref, v_ref, beta_ref, g_ref, gate_ref, on_ref, o_ref,
                  w_s, u_s, attn_s, qd_s, ktt_s, cd_s, *, nchunk):
    c = DN_CHUNK
    row = lax.broadcasted_iota(jnp.int32, (c, c), 0)
    col = lax.broadcasted_iota(jnp.int32, (c, c), 1)
    incl = (row >= col)
    strict = (row > col)
    tri_incl = incl.astype(BF16)
    eye = (row == col).astype(F32)

    def prep(grp, _):
        ns = [grp * DN_PREP_UNROLL + j for j in range(DN_PREP_UNROLL)]
        rows = [pl.ds(pl.multiple_of(n * c, c), c) for n in ns]
        each = lambda f, *ls: [f(*a) for a in zip(*ls)]
        q = [q_ref[0, r, :] for r in rows]
        k = [k_ref[0, r, :] for r in rows]
        v = [v_ref[0, r, :] for r in rows]
        beta = [beta_ref[0, r, :] for r in rows]
        g = [g_ref[0, r, :] for r in rows]
        cs = each(lambda gi: _dot_exact_lhs(
            tri_incl, jnp.concatenate([gi, jnp.where(strict, gi[:, :c], 0.0)], axis=1)), g)
        dec = [x[:, :LANES] for x in cs]
        dmask = [jnp.where(incl, jnp.exp(x[:, LANES:]), 0.0) for x in cs]
        kb = each(lambda a, bb: a * bb, k, beta)
        m = each(lambda a, bb, dm: jnp.where(strict, _bdot_nt(a, bb) * dm, 0.0), kb, k, dmask)
        x = [eye - mi for mi in m]
        ms = each(_split2, m)
        p = each(_dot3, ms, ms)
        for it in range(5):
            ps = each(_split2, p)
            x = each(lambda xi, pi: xi + _dot3(_split2(xi), pi), x, ps)
            if it < 4:
                p = each(_dot3, ps, ps)
        e = [jnp.exp(d) for d in dec]
        sol = each(lambda xi, kbi, ei, vi, bi: _dot3(
            _split2(xi), _split2(jnp.concatenate([kbi * ei, vi * bi], axis=1))), x, kb, e, v, beta)
        attn = each(lambda a, bb, dm: _bdot_nt(a, bb) * dm, q, k, dmask)
        for j, (n, r) in enumerate(zip(ns, rows)):
            w_s[r, :] = sol[j][:, :LANES]
            u_s[r, :] = sol[j][:, LANES:]
            attn_s[r, :] = attn[j]
            qd_s[r, :] = q[j] * e[j]
            dec_last = dec[j][c - 1:c, :]
            kt = k[j] * jnp.exp(dec_last - dec[j])
            ktt_s[pl.ds(pl.multiple_of(n * DN_HEAD_DIM, DN_HEAD_DIM), DN_HEAD_DIM), :] = kt.T
            cd_s[pl.ds(n, 1), :] = jnp.exp(dec_last)
        return 0

    lax.fori_loop(0, nchunk // DN_PREP_UNROLL, prep, 0)

    def step(n, state):
        r0 = pl.multiple_of(n * c, c)
        v_new = u_s[pl.ds(r0, c), :] - _bdot(w_s[pl.ds(r0, c), :], state)
        o = _bdot(qd_s[pl.ds(r0, c), :], state) + _bdot(attn_s[pl.ds(r0, c), :], v_new)
        ktt = ktt_s[pl.ds(pl.multiple_of(n * DN_HEAD_DIM, DN_HEAD_DIM), DN_HEAD_DIM), :]
        state = state * cd_s[pl.ds(n, 1), :] + _bdot(ktt, v_new)
        z = gate_ref[0, pl.ds(r0, c), :]
        y = _rms(o, on_ref[...]) * (z * _sigmoid(z))
        o_ref[0, pl.ds(r0, c), :] = y.astype(o_ref.dtype)
        return state

    lax.fori_loop(0, nchunk, step, jnp.zeros((DN_HEAD_DIM, DN_HEAD_DIM), F32))


def _delta_rule(qkv3, beta3, g3, gate3, out_norm):
    b, l, _ = qkv3.shape
    nchunk = l // DN_CHUNK
    seq = lambda off: pl.BlockSpec((1, l, LANES), lambda bi, h: (bi, 0, off + h))
    return pl.pallas_call(
        functools.partial(_delta_kernel, nchunk=nchunk),
        grid=(b, DN_HEADS),
        in_specs=[seq(0), seq(DN_HEADS), seq(2 * DN_HEADS), seq(0), seq(0), seq(0),
                  pl.BlockSpec((1, LANES), lambda bi, h: (0, 0))],
        out_specs=seq(0),
        out_shape=jax.ShapeDtypeStruct((b, l, DN_WIDTH), BF16),
        scratch_shapes=[pltpu.VMEM((l, LANES), F32), pltpu.VMEM((l, LANES), F32),
                        pltpu.VMEM((l, DN_CHUNK), F32), pltpu.VMEM((l, LANES), F32),
                        pltpu.VMEM((nchunk * DN_HEAD_DIM, DN_CHUNK), F32),
                        pltpu.VMEM((nchunk, LANES), F32)],
        compiler_params=pltpu.CompilerParams(dimension_semantics=("arbitrary", "arbitrary")),
        name="delta_rule",
    )(qkv3, qkv3, qkv3, beta3, g3, gate3, out_norm)


def _route(h, gff, wr, counts):
    tm = h.shape[0]
    xn = _rms(h, gff)
    logits = _hdot(xn, wr)
    col = lax.broadcasted_iota(jnp.int32, (tm, ROUTE_COLS), 1).astype(F32)
    neg = jnp.float32(-jnp.inf)
    first = lambda m: jnp.min(jnp.where(m, col, float(ROUTE_COLS)), axis=-1, keepdims=True)

    gl = jnp.where(col < MOE_GROUPS, logits, neg)
    gmax = jnp.max(gl, axis=-1, keepdims=True)
    gidx = first(gl == gmax)
    grp_w = 1.0 / jnp.sum(jnp.exp(gl - gmax), axis=-1, keepdims=True)

    lo = EXPERT_COL0 + MOE_PER_GROUP * gidx
    el = jnp.where((col >= lo) & (col < lo + MOE_PER_GROUP), logits, neg)
    v1 = jnp.max(el, axis=-1, keepdims=True)
    i1 = first(el == v1)
    el2 = jnp.where(col == i1, neg, el)
    v2 = jnp.max(el2, axis=-1, keepdims=True)
    i2 = first(el2 == v2)
    ex = jnp.exp(v2 - v1)
    w1 = 1.0 / (1.0 + ex)
    w2 = ex / (1.0 + ex)

    pick1 = col == i1
    pick2 = col == i2
    onehot = (pick1 | pick2).astype(BF16)
    r = lax.broadcasted_iota(jnp.int32, (tm, tm), 0)
    c = lax.broadcasted_iota(jnp.int32, (tm, tm), 1)
    before = (c < r).astype(BF16)
    prior = jnp.dot(before, onehot, preferred_element_type=F32) + counts
    rank1 = jnp.sum(jnp.where(pick1, prior, 0.0), axis=-1, keepdims=True)
    rank2 = jnp.sum(jnp.where(pick2, prior, 0.0), axis=-1, keepdims=True)
    counts = counts + jnp.sum(onehot.astype(F32), axis=0, keepdims=True)

    oc = lax.broadcasted_iota(jnp.int32, (tm, ROUTE_OUT), 1)
    e1 = i1 - float(EXPERT_COL0)
    e2 = i2 - float(EXPERT_COL0)
    vals = (e1, e2, rank1, rank2, grp_w * w1, grp_w * w2)
    info = jnp.zeros((tm, ROUTE_OUT), F32)
    for idx, val in enumerate(vals):
        info = jnp.where(oc == idx, val, info)
    return xn, info, counts


def _outproj_kernel(ysb_ref, ydn_ref, x_ref, wa_ref, wb_ref, gff_ref, wr_ref,
                    h_ref, xn_ref, info_ref, cnt_ref, cnt_s):
    @pl.when(pl.program_id(0) == 0)
    def _():
        cnt_s[...] = jnp.zeros_like(cnt_s)

    h = (x_ref[...] + jnp.dot(ysb_ref[...], wa_ref[...], preferred_element_type=F32)
         + jnp.dot(ydn_ref[...], wb_ref[...], preferred_element_type=F32))
    h_ref[...] = h
    xn, info, counts = _route(h, gff_ref[...], wr_ref[...], cnt_s[...])
    xn_ref[...] = xn
    info_ref[...] = info
    cnt_s[...] = counts
    cnt_ref[...] = counts


def _route_outs(t, d, tm):
    row = lambda n: pl.BlockSpec((tm, n), lambda i: (i, 0))
    specs = [row(d), row(d), row(ROUTE_OUT), pl.BlockSpec((1, ROUTE_COLS), lambda i: (0, 0))]
    shapes = [jax.ShapeDtypeStruct((t, d), F32), jax.ShapeDtypeStruct((t, d), F32),
              jax.ShapeDtypeStruct((t, ROUTE_OUT), F32),
              jax.ShapeDtypeStruct((1, ROUTE_COLS), F32)]
    return specs, shapes


def _outproj(ysb, ydn, x2, wa, wb, gff, wr, tm=256):
    t, d = x2.shape
    full = lambda a: pl.BlockSpec(a.shape, lambda i: (0, 0))
    row = lambda n: pl.BlockSpec((tm, n), lambda i: (i, 0))
    out_specs, out_shape = _route_outs(t, d, tm)
    return pl.pallas_call(
        _outproj_kernel,
        grid=(t // tm,),
        in_specs=[row(SB_WIDTH), row(DN_WIDTH), row(d), full(wa), full(wb), full(gff), full(wr)],
        out_specs=out_specs, out_shape=out_shape,
        scratch_shapes=[pltpu.VMEM((1, ROUTE_COLS), F32)],
        compiler_params=pltpu.CompilerParams(dimension_semantics=("arbitrary",)),
        name="outproj_route",
    )(ysb, ydn, x2, wa, wb, gff, wr)


def _dispatch_kernel(pos_ref, x_ref, xs_in, xs_out, sem, *, tm):
    del xs_in

    def copy(r, p):
        return pltpu.make_async_copy(x_ref.at[pl.ds(r, 1)], xs_out.at[pl.ds(p, 1)], sem)

    def issue(r, _):
        copy(r, pos_ref[0, 0, 2 * r]).start()
        copy(r, pos_ref[0, 0, 2 * r + 1]).start()
        return 0

    lax.fori_loop(0, tm, issue, 0, unroll=DMA_ISSUE_UNROLL)
    for _ in range(2):
        pltpu.make_async_copy(x_ref, xs_out.at[pl.ds(0, tm)], sem).wait()


def _dispatch(xn, pos, n_rows, tm=256):
    t, d = xn.shape
    pos3 = pos.reshape(t // tm, 1, 2 * tm)
    xs0 = jnp.zeros((n_rows, d), F32)
    return pl.pallas_call(
        functools.partial(_dispatch_kernel, tm=tm),
        grid=(t // tm,),
        in_specs=[pl.BlockSpec((1, 1, 2 * tm), lambda i: (i, 0, 0), memory_space=pltpu.SMEM),
                  pl.BlockSpec((tm, d), lambda i: (i, 0)),
                  pl.BlockSpec(memory_space=pl.ANY)],
        out_specs=pl.BlockSpec(memory_space=pl.ANY),
        out_shape=jax.ShapeDtypeStruct((n_rows, d), F32),
        scratch_shapes=[pltpu.SemaphoreType.DMA(())],
        input_output_aliases={2: 0},
        compiler_params=pltpu.CompilerParams(dimension_semantics=("arbitrary",)),
        name="moe_dispatch",
    )(pos3, xn, xs0)


def _gmm_kernel(te_ref, tv_ref, xs_ref, win_ref, wout_ref, ys_ref):
    i = pl.program_id(0)

    @pl.when(tv_ref[i] > 0)
    def _():
        h = jnp.dot(xs_ref[...].astype(BF16), win_ref[0], preferred_element_type=F32)
        a = h[:, :EXPERT_FF]
        v = h[:, EXPERT_FF:]
        act = (a * _sigmoid(a)) * v
        ys_ref[...] = jnp.dot(act.astype(BF16), wout_ref[0], preferred_element_type=F32)

    @pl.when(tv_ref[i] == 0)
    def _():
        ys_ref[...] = jnp.zeros_like(ys_ref)


def _gmm(xs, tile_expert, tile_valid, w_in, w_out, tm=MOE_TILE):
    n, d = xs.shape
    return pl.pallas_call(
        _gmm_kernel,
        grid_spec=pltpu.PrefetchScalarGridSpec(
            num_scalar_prefetch=2,
            grid=(n // tm,),
            in_specs=[pl.BlockSpec((tm, d), lambda i, te, tv: (i, 0)),
                      pl.BlockSpec((1, d, 2 * EXPERT_FF), lambda i, te, tv: (te[i], 0, 0)),
                      pl.BlockSpec((1, EXPERT_FF, d), lambda i, te, tv: (te[i], 0, 0))],
            out_specs=pl.BlockSpec((tm, d), lambda i, te, tv: (i, 0))),
        out_shape=jax.ShapeDtypeStruct((n, d), F32),
        compiler_params=pltpu.CompilerParams(dimension_semantics=("arbitrary",)),
        name="moe_gmm",
    )(tile_expert, tile_valid, xs, w_in, w_out)


def _combine_kernel(pos_ref, posn_ref, h_ref, gates_ref, gn_ref, ys_hbm, out_ref, xn_ref,
                    buf, sems, *, tm, nsteps):
    i = pl.program_id(0)
    slot = lax.rem(i, 2)

    def gather(p_ref, s):
        def issue(r, _):
            for k in range(2):
                pltpu.make_async_copy(ys_hbm.at[pl.ds(p_ref[0, 0, 2 * r + k], 1)],
                                      buf.at[s, k, pl.ds(r, 1)], sems.at[s]).start()
            return 0

        lax.fori_loop(0, tm, issue, 0, unroll=DMA_ISSUE_UNROLL)

    @pl.when(i == 0)
    def _():
        gather(pos_ref, 0)

    @pl.when(i + 1 < nsteps)
    def _():
        gather(posn_ref, 1 - slot)

    for k in range(2):
        pltpu.make_async_copy(ys_hbm.at[pl.ds(0, tm)], buf.at[slot, k], sems.at[slot]).wait()
    gates = gates_ref[...]
    h = h_ref[...] + gates[:, 0:1] * buf[slot, 0] + gates[:, 1:2] * buf[slot, 1]
    out_ref[...] = h
    xn_ref[...] = _rms(h, gn_ref[...])


def _combine(h, pos, gates, ys, gnext, tm=256):
    t, d = h.shape
    nsteps = t // tm
    pos3 = pos.reshape(nsteps, 1, 2 * tm)
    smem = lambda imap: pl.BlockSpec((1, 1, 2 * tm), imap, memory_space=pltpu.SMEM)
    return pl.pallas_call(
        functools.partial(_combine_kernel, tm=tm, nsteps=nsteps),
        grid=(nsteps,),
        in_specs=[smem(lambda i: (i, 0, 0)),
                  smem(lambda i: (jnp.minimum(i + 1, nsteps - 1), 0, 0)),
                  pl.BlockSpec((tm, d), lambda i: (i, 0)),
                  pl.BlockSpec((tm, 2), lambda i: (i, 0)),
                  pl.BlockSpec((1, d), lambda i: (0, 0)),
                  pl.BlockSpec(memory_space=pl.ANY)],
        out_specs=[pl.BlockSpec((tm, d), lambda i: (i, 0)), pl.BlockSpec((tm, d), lambda i: (i, 0))],
        out_shape=[jax.ShapeDtypeStruct((t, d), F32), jax.ShapeDtypeStruct((t, d), F32)],
        scratch_shapes=[pltpu.VMEM((2, 2, tm, d), F32), pltpu.SemaphoreType.DMA((2,))],
        compiler_params=pltpu.CompilerParams(dimension_semantics=("arbitrary",)),
        name="moe_combine",
    )(pos3, pos3, h, gates, gnext, ys)


def _moe_plan(info, counts, tm=MOE_TILE):
    t = info.shape[0]
    n_rows = 2 * t + N_EXPERTS * tm
    cnt = counts[0, EXPERT_COL0:EXPERT_COL0 + N_EXPERTS].astype(jnp.int32)
    padded = ((cnt + tm - 1) // tm) * tm
    ends = jnp.cumsum(padded)
    offs = ends - padded
    e = info[:, 0:2].astype(jnp.int32)
    rank = info[:, 2:4].astype(jnp.int32)
    expert_ids = jnp.arange(N_EXPERTS, dtype=jnp.int32)
    pos = jnp.sum(jnp.where(e[..., None] == expert_ids, offs, 0), axis=-1) + rank
    tile_start = jnp.arange(n_rows // tm, dtype=jnp.int32) * tm
    te = jnp.sum((tile_start[:, None] >= ends[None, :]).astype(jnp.int32), axis=1)
    tv = (te < N_EXPERTS).astype(jnp.int32)
    te = jnp.minimum(te, N_EXPERTS - 1)
    return pos, info[:, 4:6], te, tv, n_rows


def _moe(h, xn, info, counts, w_in, w_out, gnext, perm=None):
    pos, gates, te, tv, n_rows = _moe_plan(info, counts)
    xs = _dispatch(xn, pos, n_rows)
    ys = _gmm(xs, te, tv, w_in, w_out)
    if perm is not None:
        h, pos, gates = perm(h), perm(pos), perm(gates)
    return _combine(h, pos, gates, ys, gnext)


def _s5_kernel(u_ref, bmat_ref, cmat_ref, a_ref, dskip_ref, y_ref, h_s, st_s, *, nt, nb):
    tc = pl.program_id(1)
    half = h_s.shape[1] // 2

    @pl.when(tc == 0)
    def _():
        st_s[...] = jnp.zeros_like(st_s)

    u = u_ref[...]
    h_s[...] = jnp.dot(u.astype(BF16), bmat_ref[0], preferred_element_type=F32)
    a_re = jnp.broadcast_to(a_ref[0, 0:1, :], (nb, half))
    a_im = jnp.broadcast_to(a_ref[0, 1:2, :], (nb, half))

    def step(t, carry):
        hr, hi = carry
        r0 = pl.multiple_of(t * nb, nb)
        br = h_s[pl.ds(r0, nb), pl.ds(0, half)]
        bi = h_s[pl.ds(r0, nb), pl.ds(half, half)]
        nr = a_re * hr - a_im * hi + br
        ni = a_re * hi + a_im * hr + bi
        h_s[pl.ds(r0, nb), pl.ds(0, half)] = nr
        h_s[pl.ds(r0, nb), pl.ds(half, half)] = ni
        return nr, ni

    hr, hi = lax.fori_loop(0, nt, step, (st_s[0], st_s[1]), unroll=8)
    st_s[0] = hr
    st_s[1] = hi
    y = jnp.dot(h_s[...].astype(BF16), cmat_ref[0], preferred_element_type=F32)
    y = y + dskip_ref[0] * u
    y = 0.5 * y * (1.0 + jnp.tanh(math.sqrt(2.0 / math.pi) * (y + 0.044715 * (y * y * y))))
    y_ref[...] = y.astype(y_ref.dtype)


def _s5(u2, bmat, cmat, amat, dskip3, nb, nt=64):
    t, d = u2.shape
    ncc = d // LANES
    width = bmat.shape[2]
    rows = nt * nb
    return pl.pallas_call(
        functools.partial(_s5_kernel, nt=nt, nb=nb),
        grid=(ncc, t // rows),
        in_specs=[pl.BlockSpec((rows, LANES), lambda cc, tc: (tc, cc)),
                  pl.BlockSpec((1, LANES, width), lambda cc, tc: (cc, 0, 0)),
                  pl.BlockSpec((1, width, LANES), lambda cc, tc: (cc, 0, 0)),
                  pl.BlockSpec((1, 2, width // 2), lambda cc, tc: (cc, 0, 0)),
                  pl.BlockSpec((1, 1, LANES), lambda cc, tc: (cc, 0, 0))],
        out_specs=pl.BlockSpec((rows, LANES), lambda cc, tc: (tc, cc)),
        out_shape=jax.ShapeDtypeStruct((t, d), BF16),
        scratch_shapes=[pltpu.VMEM((rows, width), F32), pltpu.VMEM((2, nb, width // 2), F32)],
        compiler_params=pltpu.CompilerParams(dimension_semantics=("arbitrary", "arbitrary")),
        name="s5_scan",
    )(u2, bmat, cmat, amat, dskip3)


def _glu_kernel(y_ref, h_ref, wv_ref, wg_ref, bv_ref, bg_ref, gff_ref, wr_ref,
                hn_ref, xn_ref, info_ref, cnt_ref, cnt_s):
    @pl.when(pl.program_id(0) == 0)
    def _():
        cnt_s[...] = jnp.zeros_like(cnt_s)

    y = y_ref[...]
    zv = jnp.dot(y, wv_ref[...], preferred_element_type=F32) + bv_ref[...]
    zg = jnp.dot(y, wg_ref[...], preferred_element_type=F32) + bg_ref[...]
    h = h_ref[...] + zv * _sigmoid(zg)
    hn_ref[...] = h
    xn, info, counts = _route(h, gff_ref[...], wr_ref[...], cnt_s[...])
    xn_ref[...] = xn
    info_ref[...] = info
    cnt_s[...] = counts
    cnt_ref[...] = counts


def _glu(y, h, wv, wg, bv, bg, gff, wr, tm=256):
    t, d = h.shape
    full = lambda a: pl.BlockSpec(a.shape, lambda i: (0, 0))
    row = lambda n: pl.BlockSpec((tm, n), lambda i: (i, 0))
    out_specs, out_shape = _route_outs(t, d, tm)
    return pl.pallas_call(
        _glu_kernel,
        grid=(t // tm,),
        in_specs=[row(d), row(d), full(wv), full(wg), full(bv), full(bg), full(gff), full(wr)],
        out_specs=out_specs, out_shape=out_shape,
        scratch_shapes=[pltpu.VMEM((1, ROUTE_COLS), F32)],
        compiler_params=pltpu.CompilerParams(dimension_semantics=("arbitrary",)),
        name="glu_route",
    )(y, h, wv, wg, bv, bg, gff, wr)


def _router_weights(w_rg, w_re):
    d = w_rg.shape[0]
    pad = jnp.zeros((d, ROUTE_COLS - MOE_GROUPS - N_EXPERTS), F32)
    return jnp.concatenate([w_rg, w_re, pad], axis=1)


def _s5_params(lam_re, lam_im, log_step, b_re, b_im, c_re, c_im):
    lre = jnp.minimum(lam_re, S5_MIN_NEG)
    lim = lam_im
    dt = jnp.exp(log_step)[:, None]
    mag = jnp.exp(lre * dt)
    ab_re, ab_im = mag * jnp.cos(lim * dt), mag * jnp.sin(lim * dt)
    den = lre * lre + lim * lim
    nr, ni = ab_re - 1.0, ab_im
    coef_re = (nr * lre + ni * lim) / den
    coef_im = (ni * lre - nr * lim) / den
    bb_re = coef_re[..., None] * b_re - coef_im[..., None] * b_im
    bb_im = coef_re[..., None] * b_im + coef_im[..., None] * b_re
    gpc = LANES // S5_GROUP
    ncc = S5_GROUPS // gpc
    eye = jnp.eye(gpc, dtype=F32)

    def in_block(bb):
        x = bb.reshape(ncc, gpc, S5_STATE, S5_GROUP)
        x = jnp.einsum('ngpc,gh->ngchp', x, eye)
        return x.reshape(ncc, LANES, gpc * S5_STATE)

    def out_block(cc):
        x = cc.reshape(ncc, gpc, S5_GROUP, S5_STATE)
        x = jnp.einsum('ngcp,gh->ngphc', x, eye)
        return x.reshape(ncc, gpc * S5_STATE, LANES)

    bmat = jnp.concatenate([in_block(bb_re), in_block(bb_im)], axis=2).astype(BF16)
    cmat = jnp.concatenate([out_block(c_re), out_block(-c_im)], axis=1).astype(BF16)
    amat = jnp.stack([ab_re.reshape(ncc, gpc * S5_STATE), ab_im.reshape(ncc, gpc * S5_STATE)],
                     axis=1)
    return bmat, cmat, amat


def kernel(x, mix_norm_even, w_in_even, dn_conv_w, dn_a_log, dn_dt_bias, dn_out_norm, w_out_even, mix_norm_odd, s5_lambda_re, s5_lambda_im, s5_log_step, s5_b_re, s5_b_im, s5_c_re, s5_c_im, s5_d, w_glu, b_glu, ffn_norm, w_router_group, w_router_expert, w_expert_in, w_expert_out, final_norm):
    b, l, d = x.shape
    t = b * l
    x2 = x.reshape(t, d)

    w_in = w_in_even[0]
    c_dn = 3 * SB_WIDTH
    c_beta = c_dn + 3 * DN_WIDTH
    c_a = c_beta + DN_HEADS
    c_gate = c_a + DN_HEADS
    rep = lambda w: jnp.repeat(w, DN_HEAD_DIM, axis=1)
    wbg = jnp.concatenate([rep(w_in[:, c_beta:c_a]), rep(w_in[:, c_a:c_gate])], axis=1)
    sb, dn, gate, beta_b, g_b = _inproj(
        x2, mix_norm_even[0][None], w_in[:, :c_dn].astype(BF16),
        w_in[:, c_dn:c_beta].astype(BF16), w_in[:, c_gate:].astype(BF16), wbg.astype(BF16),
        jnp.repeat(dn_a_log[0], DN_HEAD_DIM)[None], jnp.repeat(dn_dt_bias[0], DN_HEAD_DIM)[None])
    y_sb = _sb_attention(sb.reshape(b, l, 3 * SB_WIDTH))
    qkv = _dn_conv(dn.reshape(b, l, 3 * DN_WIDTH), dn_conv_w[0])
    y_dn = _delta_rule(qkv, beta_b.reshape(b, l, DN_WIDTH), g_b.reshape(b, l, DN_WIDTH),
                       gate.reshape(b, l, DN_WIDTH), dn_out_norm[0][None])
    w_out = w_out_even[0].astype(BF16)
    h1, xn, info, counts = _outproj(
        y_sb.reshape(t, SB_WIDTH), y_dn.reshape(t, DN_WIDTH), x2, w_out[:SB_WIDTH],
        w_out[SB_WIDTH:], ffn_norm[0][None], _router_weights(w_router_group[0], w_router_expert[0]))

    to_time_major = lambda a: a.reshape(b, l, -1).transpose(1, 0, 2).reshape(t, -1)
    h2, un = _moe(h1, xn, info, counts, w_expert_in[0].astype(BF16), w_expert_out[0].astype(BF16),
                  mix_norm_odd[0][None], perm=to_time_major)

    bmat, cmat, amat = _s5_params(s5_lambda_re[0], s5_lambda_im[0], s5_log_step[0], s5_b_re[0],
                                  s5_b_im[0], s5_c_re[0], s5_c_im[0])
    y = _s5(un, bmat, cmat, amat, s5_d[0].reshape(d // LANES, 1, LANES), nb=b)
    wg = w_glu[0].astype(BF16)
    h3, xn, info, counts = _glu(
        y, h2, wg[:, :d], wg[:, d:], b_glu[0][None, :d], b_glu[0][None, d:], ffn_norm[1][None],
        _router_weights(w_router_group[1], w_router_expert[1]))
    _, out = _moe(h3, xn, info, counts, w_expert_in[1].astype(BF16), w_expert_out[1].astype(BF16),
                  final_norm[None])
    return out.reshape(l, b, d).transpose(1, 0, 2)
```
